```python
import math
import jax, jax.numpy as jnp
from jax import lax
import numpy as np

D_MODEL = 1024
BATCH = 8
SEQ = 2048
DEPTH = 2
DEC_BATCH = 128
DEC_SEQ = 4
PAST_LEN = 16384
PAGE_SIZE = 128

MIX_WIDTH = D_MODEL // 2
N_BRANCH = 3
POOL_WINDOWS = (2, 4, 8, 16)
POOL_GROUPS = len(POOL_WINDOWS)
POOL_BUF = max(POOL_WINDOWS) - 1
DN_HEADS = 4
DN_HEAD_DIM = MIX_WIDTH // DN_HEADS
DN_CONV = 4
DN_CHUNK = 64
CM_WIDTH = MIX_WIDTH
CM_KERNEL = 31
D_FF = ((8 * D_MODEL // 3 + 127) // 128) * 128
PLE_DIM = 256
RMS_EPS = 1e-6
LN_EPS = 1e-5
IN_SPLITS = (MIX_WIDTH, 4 * MIX_WIDTH, 5 * MIX_WIDTH, 5 * MIX_WIDTH + DN_HEADS,
             5 * MIX_WIDTH + 2 * DN_HEADS, 5 * MIX_WIDTH + 2 * DN_HEADS + 2 * CM_WIDTH)
IN_WIDTH = IN_SPLITS[-1] + N_BRANCH * D_MODEL

kernel_name = 'hybrid_pool_gdn_conformer_step'


def rmsnorm(x, g, eps=RMS_EPS):
    xf = x.astype(jnp.float32)
    y = xf * lax.rsqrt(jnp.mean(xf * xf, axis=-1, keepdims=True) + eps) * g.astype(jnp.float32)
    return y.astype(x.dtype)


def swiglu(x, w_up, w_down):
    a, b = jnp.split(x @ w_up, 2, axis=-1)
    return (jax.nn.silu(a) * b) @ w_down


def l2norm(x, eps=1e-6):
    return x * lax.rsqrt(jnp.sum(x * x, axis=-1, keepdims=True) + eps)


def causal_dwconv(prefix, x, w, b=None):
    width, ch = w.shape
    ext = jnp.concatenate([prefix.astype(x.dtype), x], axis=1)
    y = lax.conv_general_dilated(ext, w.astype(x.dtype)[:, None, :], (1,), 'VALID',
                                 dimension_numbers=('NWC', 'WIO', 'NWC'), feature_group_count=ch)
    if b is not None:
        y = y + b.astype(x.dtype)
    return y, ext[:, ext.shape[1] - (width - 1):]


def multiscale_pool(prefix, x, pos0, w_grp, scale):
    B, L, C = x.shape
    P = prefix.shape[1]
    gw = C // POOL_GROUPS
    ext = jnp.concatenate([prefix.astype(x.dtype), x], axis=1)
    cs = jnp.cumsum(ext.astype(jnp.float32), axis=1)
    cs0 = jnp.concatenate([jnp.zeros((B, 1, C), jnp.float32), cs], axis=1)
    pos = pos0 + jnp.arange(L)
    means = []
    for gi, w in enumerate(POOL_WINDOWS):
        sl = slice(gi * gw, (gi + 1) * gw)
        s = cs0[:, P + 1:P + 1 + L, sl] - cs0[:, P + 1 - w:P + 1 - w + L, sl]
        cnt = jnp.minimum(pos + 1, w).astype(jnp.float32)[None, :, None]
        means.append(s / cnt)
    mean = jnp.concatenate(means, axis=-1)
    d = (mean - x.astype(jnp.float32)).astype(x.dtype).reshape(B, L, POOL_GROUPS, gw)
    y = jnp.einsum('blgc,gce->blge', d, w_grp).reshape(B, L, C) * scale
    return y, ext[:, ext.shape[1] - P:]


def gated_delta_chunked(q, k, v, g, beta, S0):
    B, L, H, DK = q.shape
    DV = v.shape[-1]
    C = min(DN_CHUNK, L)
    n = -(-L // C)
    pad = n * C - L

    def prep(t):
        t = jnp.pad(t, [(0, 0), (0, pad)] + [(0, 0)] * (t.ndim - 2))
        t = jnp.moveaxis(t, 2, 1)
        return t.reshape(t.shape[:2] + (n, C) + t.shape[3:])

    q, k, v, g, beta = prep(q), prep(k), prep(v), prep(g), prep(beta)
    gc = jnp.cumsum(g, axis=-1)
    idx = jnp.arange(C)
    incl = idx[:, None] >= idx[None, :]
    strict = idx[:, None] > idx[None, :]
    decay = jnp.exp(jnp.where(incl, gc[..., :, None] - gc[..., None, :], -jnp.inf))
    kb = k * beta[..., None]
    lower = jnp.where(strict, jnp.einsum('bhnik,bhnjk->bhnij', kb, k) * decay, 0.0)
    eye = jnp.eye(C, dtype=jnp.float32)
    T = lax.linalg.triangular_solve(eye + lower, jnp.broadcast_to(eye, lower.shape),
                                    left_side=True, lower=True, unit_diagonal=True)
    u = jnp.einsum('bhnij,bhnjv->bhniv', T, v * beta[..., None])
    w = jnp.einsum('bhnij,bhnjk->bhnik', T, kb * jnp.exp(gc)[..., None])
    attn = jnp.einsum('bhnik,bhnjk->bhnij', q, k) * decay
    g_last = gc[..., -1]
    k_tail = k * jnp.exp(g_last[..., None] - gc)[..., None]
    q_dec = q * jnp.exp(gc)[..., None]

    def step(S, xs):
        qd, wc, uc, ac, kt, gl = xs
        v_new = uc - jnp.einsum('bhck,bhkv->bhcv', wc, S)
        o = jnp.einsum('bhck,bhkv->bhcv', qd, S) + jnp.einsum('bhij,bhjv->bhiv', ac, v_new)
        S = S * jnp.exp(gl)[..., None, None] + jnp.einsum('bhck,bhcv->bhkv', kt, v_new)
        return S, o

    xs = tuple(jnp.moveaxis(t, 2, 0) for t in (q_dec, w, u, attn, k_tail, g_last))
    S, o = lax.scan(step, S0, xs)
    o = jnp.moveaxis(o, 0, 2).reshape(B, H, n * C, DV)[:, :, :L]
    return jnp.moveaxis(o, 1, 2), S


def gated_deltanet(prefix_conv, S0, qkv, z, a, b, conv_w, A_log, dt_bias, norm_g):
    B, L, _ = qkv.shape
    qkv_c, new_conv = causal_dwconv(prefix_conv, qkv, conv_w)
    qkv_c = jax.nn.silu(qkv_c.astype(jnp.float32))
    q, k, v = jnp.split(qkv_c, 3, axis=-1)
    q = l2norm(q.reshape(B, L, DN_HEADS, DN_HEAD_DIM)) * (DN_HEAD_DIM ** -0.5)
    k = l2norm(k.reshape(B, L, DN_HEADS, DN_HEAD_DIM))
    v = v.reshape(B, L, DN_HEADS, DN_HEAD_DIM)
    g = -jnp.exp(A_log.astype(jnp.float32)) * jax.nn.softplus(a.astype(jnp.float32) + dt_bias.astype(jnp.float32))
    beta = jax.nn.sigmoid(b.astype(jnp.float32))
    o, S = gated_delta_chunked(q, k, v, g, beta, S0.astype(jnp.float32))
    o = o * lax.rsqrt(jnp.mean(o * o, axis=-1, keepdims=True) + RMS_EPS) * norm_g.astype(jnp.float32)
    o = o.reshape(B, L, DN_HEADS * DN_HEAD_DIM) * jax.nn.silu(z.astype(jnp.float32))
    return o.astype(qkv.dtype), new_conv, S


def conformer_conv(prefix, glu_in, dw_w, dw_b, ln_g, ln_b):
    a, gate = jnp.split(glu_in, 2, axis=-1)
    h = a * jax.nn.sigmoid(gate)
    y, new_buf = causal_dwconv(prefix, h, dw_w, dw_b)
    yf = y.astype(jnp.float32)
    mu = jnp.mean(yf, axis=-1, keepdims=True)
    var = jnp.mean(jnp.square(yf - mu), axis=-1, keepdims=True)
    yn = (yf - mu) * lax.rsqrt(var + LN_EPS) * ln_g.astype(jnp.float32) + ln_b.astype(jnp.float32)
    return jax.nn.silu(yn).astype(glu_in.dtype), new_buf


def decoder_layer(x, p, pos0, st_pool, st_dnconv, st_dn, st_cm, lw):
    B, L, D = x.shape
    x = x + 0.5 * swiglu(rmsnorm(x, lw['g_ffn1']), lw['w_ffn1_up'], lw['w_ffn1_down'])
    u = rmsnorm(x, lw['g_mix'])
    proj = u @ lw['w_in']
    x_pool, qkv, z, a, b, glu_in, gates = jnp.split(proj, IN_SPLITS, axis=-1)
    o_pool, new_pool = multiscale_pool(st_pool, x_pool, pos0, lw['pool_w'], lw['pool_scale'])
    o_dn, new_dnconv, new_dn = gated_deltanet(st_dnconv, st_dn, qkv, z, a, b, lw['dn_conv_w'],
                                              lw['dn_A_log'], lw['dn_dt_bias'], lw['dn_norm_g'])
    o_cm, new_cm = conformer_conv(st_cm, glu_in, lw['cm_dw_w'], lw['cm_dw_b'], lw['cm_ln_g'], lw['cm_ln_b'])
    br = jnp.stack([o_pool, o_dn, o_cm], axis=2)
    br = jnp.einsum('blnc,ncd->blnd', br, lw['w_branch'])
    gt = jax.nn.sigmoid(gates.astype(jnp.float32)).reshape(B, L, N_BRANCH, D)
    merged = jnp.sum(gt * br.astype(jnp.float32), axis=2).astype(x.dtype)
    x = x + merged @ lw['w_out']
    x = x + 0.5 * swiglu(rmsnorm(x, lw['g_ffn2']), lw['w_ffn2_up'], lw['w_ffn2_down'])
    pe = p.astype(x.dtype) @ lw['w_ple_proj']
    x = x + jax.nn.sigmoid(rmsnorm(x, lw['g_ple']) @ lw['w_ple_gate']) * pe
    return x, new_pool, new_dnconv, new_dn, new_cm


def setup_inputs(seed: int = 0) -> dict:
    key = jax.random.key(seed)
    ks = iter(jax.random.split(key, 40))
    f32 = jnp.float32

    def nrm(shape, scale):
        return jax.random.normal(next(ks), shape, f32) * scale

    def gain(shape):
        return 1.0 + nrm(shape, 0.05)

    L = DEPTH
    x_prompt = nrm((BATCH, SEQ, D_MODEL), 1.0)
    x_sample = nrm((DEC_BATCH, DEC_SEQ, D_MODEL), 1.0)
    state_pool = nrm((L, DEC_BATCH, POOL_BUF, MIX_WIDTH), 1.0)
    state_dn_conv = nrm((L, DEC_BATCH, DN_CONV - 1, 3 * MIX_WIDTH), 1.0)
    state_dn = nrm((L, DEC_BATCH, DN_HEADS, DN_HEAD_DIM, DN_HEAD_DIM), 0.1)
    state_cm_conv = nrm((L, DEC_BATCH, CM_KERNEL - 1, CM_WIDTH), 0.5)
    p_prompt = nrm((L, BATCH, SEQ, PLE_DIM), 1.0)
    p_sample = nrm((L, DEC_BATCH, DEC_SEQ, PLE_DIM), 1.0)
    g_ffn1 = gain((L, D_MODEL))
    w_ffn1_up = nrm((L, D_MODEL, 2 * D_FF), D_MODEL ** -0.5)
    w_ffn1_down = nrm((L, D_FF, D_MODEL), D_FF ** -0.5)
    g_mix = gain((L, D_MODEL))
    w_in = nrm((L, D_MODEL, IN_WIDTH), D_MODEL ** -0.5)
    pool_w = nrm((L, POOL_GROUPS, MIX_WIDTH // POOL_GROUPS, MIX_WIDTH // POOL_GROUPS), (MIX_WIDTH // POOL_GROUPS) ** -0.5)
    pool_scale = gain((L, MIX_WIDTH))
    dn_conv_w = nrm((L, DN_CONV, 3 * MIX_WIDTH), DN_CONV ** -0.5)
    dn_A_log = jnp.log(jax.random.uniform(next(ks), (L, DN_HEADS), f32, 1.0, 16.0))
    dt = jnp.exp(jax.random.uniform(next(ks), (L, DN_HEADS), f32, math.log(1e-3), math.log(1e-1)))
    dn_dt_bias = dt + jnp.log(-jnp.expm1(-dt))
    dn_norm_g = gain((L, DN_HEAD_DIM))
    cm_dw_w = nrm((L, CM_KERNEL, CM_WIDTH), CM_KERNEL ** -0.5)
    cm_dw_b = nrm((L, CM_WIDTH), 0.01)
    cm_ln_g = gain((L, CM_WIDTH))
    cm_ln_b = nrm((L, CM_WIDTH), 0.01)
    w_branch = nrm((L, N_BRANCH, MIX_WIDTH, D_MODEL), MIX_WIDTH ** -0.5)
    w_out = nrm((L, D_MODEL, D_MODEL), D_MODEL ** -0.5)
    g_ffn2 = gain((L, D_MODEL))
    w_ffn2_up = nrm((L, D_MODEL, 2 * D_FF), D_MODEL ** -0.5)
    w_ffn2_down = nrm((L, D_FF, D_MODEL), D_FF ** -0.5)
    g_ple = gain((L, D_MODEL))
    w_ple_gate = nrm((L, D_MODEL, D_MODEL), D_MODEL ** -0.5)
    w_ple_proj = nrm((L, PLE_DIM, D_MODEL), PLE_DIM ** -0.5)
    g_final = gain((D_MODEL,))
    return {'x_prompt': x_prompt, 'x_sample': x_sample, 'state_pool': state_pool,
            'state_dn_conv': state_dn_conv, 'state_dn': state_dn, 'state_cm_conv': state_cm_conv,
            'p_prompt': p_prompt, 'p_sample': p_sample,
            'g_ffn1': g_ffn1, 'w_ffn1_up': w_ffn1_up, 'w_ffn1_down': w_ffn1_down,
            'g_mix': g_mix, 'w_in': w_in, 'pool_w': pool_w, 'pool_scale': pool_scale,
            'dn_conv_w': dn_conv_w, 'dn_A_log': dn_A_log, 'dn_dt_bias': dn_dt_bias, 'dn_norm_g': dn_norm_g,
            'cm_dw_w': cm_dw_w, 'cm_dw_b': cm_dw_b, 'cm_ln_g': cm_ln_g, 'cm_ln_b': cm_ln_b,
            'w_branch': w_branch, 'w_out': w_out,
            'g_ffn2': g_ffn2, 'w_ffn2_up': w_ffn2_up, 'w_ffn2_down': w_ffn2_down,
            'g_ple': g_ple, 'w_ple_gate': w_ple_gate, 'w_ple_proj': w_ple_proj, 'g_final': g_final}


def reference(x_prompt, x_sample, state_pool, state_dn_conv, state_dn, state_cm_conv, p_prompt, p_sample,
              g_ffn1, w_ffn1_up, w_ffn1_down, g_mix, w_in, pool_w, pool_scale,
              dn_conv_w, dn_A_log, dn_dt_bias, dn_norm_g, cm_dw_w, cm_dw_b, cm_ln_g, cm_ln_b,
              w_branch, w_out, g_ffn2, w_ffn2_up, w_ffn2_down, g_ple, w_ple_gate, w_ple_proj, g_final):
    bp = x_prompt.shape[0]
    dt = x_prompt.dtype
    pool0 = jnp.zeros((bp, POOL_BUF, MIX_WIDTH), dt)
    dnconv0 = jnp.zeros((bp, DN_CONV - 1, 3 * MIX_WIDTH), dt)
    dn0 = jnp.zeros((bp, DN_HEADS, DN_HEAD_DIM, DN_HEAD_DIM), jnp.float32)
    cm0 = jnp.zeros((bp, CM_KERNEL - 1, CM_WIDTH), dt)
    hp, hs = x_prompt, x_sample
    pool_p, dnconv_p, dn_p, cm_p = [], [], [], []
    pool_s, dnconv_s, dn_s, cm_s = [], [], [], []
    for i in range(DEPTH):
        lw = {'g_ffn1': g_ffn1[i], 'w_ffn1_up': w_ffn1_up[i], 'w_ffn1_down': w_ffn1_down[i],
              'g_mix': g_mix[i], 'w_in': w_in[i], 'pool_w': pool_w[i], 'pool_scale': pool_scale[i],
              'dn_conv_w': dn_conv_w[i], 'dn_A_log': dn_A_log[i], 'dn_dt_bias': dn_dt_bias[i],
              'dn_norm_g': dn_norm_g[i], 'cm_dw_w': cm_dw_w[i], 'cm_dw_b': cm_dw_b[i],
              'cm_ln_g': cm_ln_g[i], 'cm_ln_b': cm_ln_b[i], 'w_branch': w_branch[i], 'w_out': w_out[i],
              'g_ffn2': g_ffn2[i], 'w_ffn2_up': w_ffn2_up[i], 'w_ffn2_down': w_ffn2_down[i],
              'g_ple': g_ple[i], 'w_ple_gate': w_ple_gate[i], 'w_ple_proj': w_ple_proj[i]}
        hp, a1, a2, a3, a4 = decoder_layer(hp, p_prompt[i], 0, pool0, dnconv0, dn0, cm0, lw)
        hs, b1, b2, b3, b4 = decoder_layer(hs, p_sample[i], PAST_LEN, state_pool[i], state_dn_conv[i],
                                           state_dn[i], state_cm_conv[i], lw)
        pool_p.append(a1); dnconv_p.append(a2); dn_p.append(a3); cm_p.append(a4)
        pool_s.append(b1); dnconv_s.append(b2); dn_s.append(b3); cm_s.append(b4)
    y_prompt = rmsnorm(hp, g_final)
    y_sample = rmsnorm(hs, g_final)
    return (y_prompt, y_sample,
            jnp.stack(pool_p), jnp.stack(dnconv_p), jnp.stack(dn_p), jnp.stack(cm_p),
            jnp.stack(pool_s), jnp.stack(dnconv_s), jnp.stack(dn_s), jnp.stack(cm_s))
```

```python
import functools
import math

import jax
import jax.numpy as jnp
import numpy as np
from jax import lax
from jax.experimental import pallas as pl
from jax.experimental.pallas import tpu as pltpu

F32 = jnp.float32
BF16 = jnp.bfloat16

D_MODEL = 1024
MIX = 512
HEADS = 4
HEAD_DIM = 128
POOL_WINDOWS = (2, 4, 8, 16)
POOL_BUF = 15
DN_TAPS = 4
CM_TAPS = 31
D_FF = 2816
PLE_DIM = 256
PAST_LEN = 16384
RMS_EPS = 1e-6
LN_EPS = 1e-5

OFF_POOL = 0
OFF_QKV = 512
OFF_Z = 2048
OFF_AB = 2560
OFF_GLU = 2688
OFF_GATE = 3712
IN_PACKED = OFF_GATE + 3 * D_MODEL

LANES = 128
SUBLANES = 8
VMEM_LIMIT = 56 * 1024 * 1024

FFN_TILE = 512
PROMPT_TILE = 256
DN_CHUNK = 64
SAMPLE_SEQS = 16
SAMPLE_PAD = 8


def _bdot(a, b):
    return jnp.dot(a.astype(BF16), b.astype(BF16), preferred_element_type=F32)


def _bdot_nt(a, b):
    return lax.dot_general(a.astype(BF16), b.astype(BF16), (((1,), (1,)), ((), ())),
                           preferred_element_type=F32)


def _rms(x, g, eps=RMS_EPS):
    return x * lax.rsqrt(jnp.mean(x * x, axis=-1, keepdims=True) + eps) * g


def _silu(x):
    return x * jax.nn.sigmoid(x)


def _softplus(x):
    return jnp.maximum(x, 0.0) + jnp.log(1.0 + jnp.exp(-jnp.abs(x)))


def _resident(shape):
    nd = len(shape)
    return pl.BlockSpec(shape, lambda *_: (0,) * nd, pipeline_mode=pl.Buffered(1))


def _swiglu_half(x, g_ref, wup_ref, wdn_ref):
    xn = _rms(x, g_ref[...]).astype(BF16)
    a = jnp.dot(xn, wup_ref[:, :D_FF], preferred_element_type=F32)
    b = jnp.dot(xn, wup_ref[:, D_FF:], preferred_element_type=F32)
    h = (_silu(a) * b).astype(BF16)
    return x + 0.5 * jnp.dot(h, wdn_ref[...], preferred_element_type=F32)


def _ffn_kernel(x_ref, g_ref, wup_ref, wdn_ref, o_ref):
    o_ref[...] = _swiglu_half(x_ref[...], g_ref, wup_ref, wdn_ref)


def _ffn_ple_kernel(x_ref, p_ref, g_ref, wup_ref, wdn_ref, gple_ref, wgate_ref, wproj_ref, gfin_ref, o_ref,
                    *, final):
    x = _swiglu_half(x_ref[...], g_ref, wup_ref, wdn_ref)
    pe = _bdot(p_ref[...], wproj_ref[...])
    gate = jax.nn.sigmoid(_bdot(_rms(x, gple_ref[...]), wgate_ref[...]))
    x = x + gate * pe
    if final:
        x = _rms(x, gfin_ref[...])
    o_ref[...] = x


def _row_tile(n):
    return FFN_TILE if n % FFN_TILE == 0 else n


def _ffn(x, g, wup, wdn):
    n = x.shape[0]
    tm = _row_tile(n)
    row = pl.BlockSpec((tm, D_MODEL), lambda i: (i, 0))
    return pl.pallas_call(
        _ffn_kernel,
        grid=(n // tm,),
        in_specs=[row, _resident(g.shape), _resident(wup.shape), _resident(wdn.shape)],
        out_specs=row,
        out_shape=jax.ShapeDtypeStruct(x.shape, F32),
        compiler_params=pltpu.CompilerParams(dimension_semantics=("arbitrary",), vmem_limit_bytes=VMEM_LIMIT),
        name="ffn",
    )(x, g, wup, wdn)


def _ffn_ple(x, p, g, wup, wdn, gple, wgate, wproj, gfin, final):
    n = x.shape[0]
    tm = _row_tile(n)
    row = pl.BlockSpec((tm, D_MODEL), lambda i: (i, 0))
    prow = pl.BlockSpec((tm, PLE_DIM), lambda i: (i, 0))
    return pl.pallas_call(
        functools.partial(_ffn_ple_kernel, final=final),
        grid=(n // tm,),
        in_specs=[row, prow, _resident(g.shape), _resident(wup.shape), _resident(wdn.shape),
                  _resident(gple.shape), _resident(wgate.shape), _resident(wproj.shape), _resident(gfin.shape)],
        out_specs=row,
        out_shape=jax.ShapeDtypeStruct(x.shape, F32),
        compiler_params=pltpu.CompilerParams(dimension_semantics=("arbitrary",), vmem_limit_bytes=VMEM_LIMIT),
        name="ffn_ple",
    )(x, p, g, wup, wdn, gple, wgate, wproj, gfin)


def _delta_intra(q, k, v, g, beta, blk):
    r = q.shape[0]
    shift = int(math.log2(blk))
    ri = lax.broadcasted_iota(jnp.int32, (r, r), 0)
    ci = lax.broadcasted_iota(jnp.int32, (r, r), 1)
    same = (ri >> shift) == (ci >> shift)
    incl = same & (ri >= ci)
    strict = same & (ri > ci)
    sums = jnp.dot(jnp.concatenate([jnp.where(incl, 1.0, 0.0), jnp.where(same, 1.0, 0.0)], axis=0), g,
                   precision=lax.Precision.HIGHEST, preferred_element_type=F32)
    gc = sums[:r]
    g_last = sums[r:]
    gc_sq = gc if r == LANES else jnp.concatenate([gc] * (r // LANES), axis=1)
    decay = jnp.exp(jnp.where(incl, gc_sq - gc_sq.T, -1e30))
    kb = k * beta
    kq = _bdot_nt(jnp.concatenate([kb, q], axis=0), k)
    n = jnp.where(strict, kq[:r] * decay, 0.0)
    attn = kq[r:] * decay
    p = -n
    t = jnp.where(ri == ci, 1.0, 0.0) + p
    for _ in range(shift - 1):
        p = _bdot(p, p)
        t = t + _bdot(t, p)
    eg = jnp.exp(gc)
    uw = _bdot(t, jnp.concatenate([v * beta, kb * eg], axis=1))
    u = uw[:, :HEAD_DIM]
    w = uw[:, HEAD_DIM:]
    k_tail = k * jnp.exp(g_last - gc)
    q_dec = q * eg
    return u, w, attn, k_tail, q_dec, jnp.exp(g_last)


def _l2norm(x):
    return x * lax.rsqrt(jnp.sum(x * x, axis=-1, keepdims=True) + 1e-6)


def _decay_beta(ab, alog_ref, dtb_ref):
    g = -jnp.exp(alog_ref[...]) * _softplus(ab + dtb_ref[...])
    beta = jax.nn.sigmoid(ab)
    return g, beta


def _pool_project(d, poolw_ref, pscale_ref):
    ys = [_bdot(d[:, gi * LANES:(gi + 1) * LANES], poolw_ref[gi]) for gi in range(len(POOL_WINDOWS))]
    return jnp.concatenate(ys, axis=1) * pscale_ref[...]


def _layernorm_swish(y, lng_ref, lnb_ref):
    mu = jnp.mean(y, axis=-1, keepdims=True)
    var = jnp.mean(jnp.square(y - mu), axis=-1, keepdims=True)
    yn = (y - mu) * lax.rsqrt(var + LN_EPS) * lng_ref[...] + lnb_ref[...]
    return _silu(yn)


def _merge_out(x, u, branches, win_ref, wbr_ref, wout_ref):
    merged = None
    for nb, o in enumerate(branches):
        br = _bdot(o, wbr_ref[nb])
        gate = jax.nn.sigmoid(jnp.dot(u, win_ref[:, OFF_GATE + nb * D_MODEL:OFF_GATE + (nb + 1) * D_MODEL],
                                      preferred_element_type=F32))
        merged = gate * br if merged is None else merged + gate * br
    return x + _bdot(merged, wout_ref[...])


def _head_norm_gate(o, ng_ref):
    return o * lax.rsqrt(jnp.mean(o * o, axis=-1, keepdims=True) + RMS_EPS) * ng_ref[...]


PPAD = 16
QPAD = 8
CPAD = 32
CONV_ROWS = 32


def _conv_rows(ext_ref, w_ref, out_ref, taps, first, rows, width, post):
    for cb in range(width // MIX):
        cs = slice(cb * MIX, (cb + 1) * MIX)
        for rb in range(rows // CONV_ROWS):
            r0 = rb * CONV_ROWS
            acc = None
            for j in range(taps):
                term = ext_ref[first + r0 + j:first + r0 + j + CONV_ROWS, cs] * w_ref[j:j + 1, cs]
                acc = term if acc is None else acc + term
            out_ref[r0:r0 + CONV_ROWS, cs] = post(acc, cs)


def _mixer_prompt_kernel(x_ref, gmix_ref, win_ref, poolw_ref, pscale_ref, dnw_ref, alog_ref, dtb_ref, ng_ref,
                         cmw_ref, cmb_ref, lng_ref, lnb_ref, wbr_ref, wout_ref,
                         xo_ref, npool_ref, ndnc_ref, ndn_ref, ncm_ref,
                         pext, qext, cext, qkvc, odn, cmy, s_ref):
    tl = PROMPT_TILE
    t_idx = pl.program_id(1)

    @pl.when(t_idx == 0)
    def _():
        pext[0:PPAD, :] = jnp.zeros((PPAD, MIX), F32)
        qext[0:QPAD, :] = jnp.zeros((QPAD, 3 * MIX), F32)
        cext[0:CPAD, :] = jnp.zeros((CPAD, MIX), F32)
        s_ref[...] = jnp.zeros(s_ref.shape, F32)

    x = x_ref[...]
    u = _rms(x, gmix_ref[...]).astype(BF16)

    pext[PPAD:PPAD + tl, :] = jnp.dot(u, win_ref[:, OFF_POOL:OFF_POOL + MIX], preferred_element_type=F32)
    pos = t_idx * tl + lax.broadcasted_iota(jnp.int32, (tl, LANES), 0)
    ds = []
    for gi, wdw in enumerate(POOL_WINDOWS):
        cs = slice(gi * LANES, (gi + 1) * LANES)
        cur = pext[PPAD:PPAD + tl, cs]
        s = cur
        for j in range(1, wdw):
            s = s + pext[PPAD - j:PPAD - j + tl, cs]
        cnt = jnp.minimum(pos + 1, wdw).astype(F32)
        ds.append(s / cnt - cur)
    o_pool = _pool_project(jnp.concatenate(ds, axis=1), poolw_ref, pscale_ref)

    qext[QPAD:QPAD + tl, :] = jnp.dot(u, win_ref[:, OFF_QKV:OFF_QKV + 3 * MIX], preferred_element_type=F32)
    _conv_rows(qext, dnw_ref, qkvc, DN_TAPS, QPAD - (DN_TAPS - 1), tl, 3 * MIX, lambda a, cs: _silu(a))
    ab = jnp.dot(u, win_ref[:, OFF_AB:OFF_AB + LANES], preferred_element_type=F32)
    g_slab, b_slab = _decay_beta(ab, alog_ref, dtb_ref)
    c = DN_CHUNK
    for ch in range(tl // c):
        rs = slice(ch * c, (ch + 1) * c)

        def stack(col0):
            return jnp.concatenate([qkvc[rs, col0 + h * HEAD_DIM:col0 + (h + 1) * HEAD_DIM]
                                    for h in range(HEADS)], axis=0)

        q = _l2norm(stack(0)) * (HEAD_DIM ** -0.5)
        k = _l2norm(stack(MIX))
        v = stack(2 * MIX)
        g = jnp.concatenate([jnp.broadcast_to(g_slab[rs, h:h + 1], (c, LANES)) for h in range(HEADS)], axis=0)
        beta = jnp.concatenate([jnp.broadcast_to(b_slab[rs, HEADS + h:HEADS + h + 1], (c, LANES))
                                for h in range(HEADS)], axis=0)
        u_, w_, attn, k_tail, q_dec, eg_last = _delta_intra(q, k, v, g, beta, c)
        ws, qs = [], []
        for h in range(HEADS):
            hs = slice(h * c, (h + 1) * c)
            wq = _bdot(jnp.concatenate([w_[hs], q_dec[hs]], axis=0), s_ref[h])
            ws.append(wq[:c])
            qs.append(wq[c:])
        v_new = u_ - jnp.concatenate(ws, axis=0)
        o = jnp.concatenate(qs, axis=0) + _bdot(attn, v_new)
        for h in range(HEADS):
            hs = slice(h * c, (h + 1) * c)
            s_ref[h] = s_ref[h] * eg_last[h * c:h * c + 1, :] + _bdot(k_tail[hs].T, v_new[hs])
        o = _head_norm_gate(o, ng_ref)
        odn[rs, :] = jnp.concatenate([o[h * c:(h + 1) * c] for h in range(HEADS)], axis=1)
    z = jnp.dot(u, win_ref[:, OFF_Z:OFF_Z + MIX], preferred_element_type=F32)
    o_dn = odn[...] * _silu(z)

    glu = jnp.dot(u, win_ref[:, OFF_GLU:OFF_GLU + 2 * MIX], preferred_element_type=F32)
    cext[CPAD:CPAD + tl, :] = glu[:, :MIX] * jax.nn.sigmoid(glu[:, MIX:])
    _conv_rows(cext, cmw_ref, cmy, CM_TAPS, CPAD - (CM_TAPS - 1), tl, MIX, lambda a, cs: a + cmb_ref[:, cs])
    o_cm = _layernorm_swish(cmy[...], lng_ref, lnb_ref)

    xo_ref[...] = _merge_out(x, u, (o_pool, o_dn, o_cm), win_ref, wbr_ref, wout_ref)

    @pl.when(t_idx == pl.num_programs(1) - 1)
    def _():
        npool_ref[...] = pext[PPAD + tl - POOL_BUF:PPAD + tl, :]
        ndnc_ref[...] = qext[QPAD + tl - (DN_TAPS - 1):QPAD + tl, :]
        ncm_ref[...] = cext[CPAD + tl - (CM_TAPS - 1):CPAD + tl, :]
        ndn_ref[...] = s_ref[...]

    pext[0:PPAD, :] = pext[tl:tl + PPAD, :]
    qext[0:QPAD, :] = qext[tl:tl + QPAD, :]
    cext[0:CPAD, :] = cext[tl:tl + CPAD, :]


def _mixer_prompt(x, lw):
    b, l, _ = x.shape
    tl = PROMPT_TILE
    xspec = pl.BlockSpec((None, tl, D_MODEL), lambda i, j: (i, j, 0))
    weights = [lw[k] for k in ("g_mix", "w_in", "pool_w", "pool_scale", "dn_conv_w", "dn_A_log", "dn_dt_bias",
                               "dn_norm_g", "cm_dw_w", "cm_dw_b", "cm_ln_g", "cm_ln_b", "w_branch", "w_out")]
    out_shapes = (jax.ShapeDtypeStruct(x.shape, F32),
                  jax.ShapeDtypeStruct((b, POOL_BUF, MIX), F32),
                  jax.ShapeDtypeStruct((b, DN_TAPS - 1, 3 * MIX), F32),
                  jax.ShapeDtypeStruct((b, HEADS, HEAD_DIM, HEAD_DIM), F32),
                  jax.ShapeDtypeStruct((b, CM_TAPS - 1, MIX), F32))
    out_specs = (xspec,
                 pl.BlockSpec((None, POOL_BUF, MIX), lambda i, j: (i, 0, 0)),
                 pl.BlockSpec((None, DN_TAPS - 1, 3 * MIX), lambda i, j: (i, 0, 0)),
                 pl.BlockSpec((None, HEADS, HEAD_DIM, HEAD_DIM), lambda i, j: (i, 0, 0, 0)),
                 pl.BlockSpec((None, CM_TAPS - 1, MIX), lambda i, j: (i, 0, 0)))
    scratch = [pltpu.VMEM((PPAD + tl, MIX), F32), pltpu.VMEM((QPAD + tl, 3 * MIX), F32),
               pltpu.VMEM((CPAD + tl, MIX), F32), pltpu.VMEM((tl, 3 * MIX), F32), pltpu.VMEM((tl, MIX), F32),
               pltpu.VMEM((tl, MIX), F32), pltpu.VMEM((HEADS, HEAD_DIM, HEAD_DIM), F32)]
    return pl.pallas_call(
        _mixer_prompt_kernel,
        grid=(b, l // tl),
        in_specs=[xspec] + [_resident(w.shape) for w in weights],
        out_specs=out_specs,
        out_shape=out_shapes,
        scratch_shapes=scratch,
        compiler_params=pltpu.CompilerParams(dimension_semantics=("arbitrary", "arbitrary"),
                                             vmem_limit_bytes=VMEM_LIMIT),
        name="mixer_prompt",
    )(x, *weights)


def _conv_state(state, x3, ws_ref, wx_ref, cnt_ref=None):
    sub = lax.broadcasted_iota(jnp.int32, x3.shape, 1)
    y = jnp.zeros(x3.shape, F32)
    for t in range(4):
        yt = (jnp.sum(state * ws_ref[t][None], axis=1, keepdims=True)
              + jnp.sum(x3 * wx_ref[t][None], axis=1, keepdims=True))
        if cnt_ref is not None:
            yt = yt / cnt_ref[t][None]
        y = jnp.where(sub == t, yt, y)
    return y


def _mixer_sample_kernel(x_ref, spool_ref, sdnc_ref, sdn_ref, scm_ref,
                         gmix_ref, win_ref, poolw_ref, pscale_ref, alog_ref, dtb_ref, ng_ref,
                         cmb_ref, lng_ref, lnb_ref, wbr_ref, wout_ref,
                         pms_ref, pmx_ref, pcnt_ref, dws_ref, dwx_ref, cws_ref, cwx_ref,
                         xo_ref, npool_ref, ndnc_ref, ndn_ref, ncm_ref,
                         w_sc, qd_sc, u_sc, vn_sc, qs_sc, kt_sc, eg_sc):
    ns = SAMPLE_SEQS
    rows = ns * SAMPLE_PAD
    x = x_ref[...]
    u = _rms(x, gmix_ref[...]).astype(BF16)
    real = (lax.broadcasted_iota(jnp.int32, (rows, LANES), 0) & (SAMPLE_PAD - 1)) < 4

    xp = jnp.dot(u, win_ref[:, OFF_POOL:OFF_POOL + MIX], preferred_element_type=F32)
    xp3 = xp.reshape(ns, SAMPLE_PAD, MIX)
    spool = spool_ref[...]
    mean3 = _conv_state(spool, xp3, pms_ref, pmx_ref, pcnt_ref)
    o_pool = _pool_project((mean3 - xp3).reshape(rows, MIX), poolw_ref, pscale_ref)
    npool_ref[:, 0:POOL_BUF - 4, :] = spool_ref[:, 4:POOL_BUF, :]
    npool_ref[:, POOL_BUF - 4:POOL_BUF, :] = xp3[:, 0:4, :]

    qkv = jnp.dot(u, win_ref[:, OFF_QKV:OFF_QKV + 3 * MIX], preferred_element_type=F32)
    qkv3 = qkv.reshape(ns, SAMPLE_PAD, 3 * MIX)
    qkvc = _silu(_conv_state(sdnc_ref[...], qkv3, dws_ref, dwx_ref)).reshape(rows, 3 * MIX)
    ndnc_ref[...] = qkv3[:, 1:4, :]
    ab = jnp.dot(u, win_ref[:, OFF_AB:OFF_AB + LANES], preferred_element_type=F32)
    g_slab, b_slab = _decay_beta(ab, alog_ref, dtb_ref)
    attns = []
    for h in range(HEADS):
        hs = slice(h * HEAD_DIM, (h + 1) * HEAD_DIM)
        q = _l2norm(qkvc[:, hs]) * (HEAD_DIM ** -0.5)
        k = _l2norm(qkvc[:, MIX + h * HEAD_DIM:MIX + (h + 1) * HEAD_DIM])
        v = qkvc[:, 2 * MIX + h * HEAD_DIM:2 * MIX + (h + 1) * HEAD_DIM]
        g = jnp.where(real, jnp.broadcast_to(g_slab[:, h:h + 1], (rows, LANES)), 0.0)
        beta = jnp.where(real, jnp.broadcast_to(b_slab[:, HEADS + h:HEADS + h + 1], (rows, LANES)), 0.0)
        u_, w_, attn, k_tail, q_dec, eg_last = _delta_intra(q, k, v, g, beta, SAMPLE_PAD)
        attns.append(attn)
        w_sc[h] = w_
        qd_sc[h] = q_dec
        u_sc[h] = u_
        kt_sc[h] = k_tail.T
        eg_sc[h] = eg_last

    def state_read(s, carry):
        r0 = pl.multiple_of(s * SAMPLE_PAD, SAMPLE_PAD)
        for h in range(HEADS):
            wq = jnp.concatenate([w_sc[h, pl.ds(r0, SAMPLE_PAD), :], qd_sc[h, pl.ds(r0, SAMPLE_PAD), :]], axis=0)
            r = _bdot(wq, sdn_ref[s, h])
            vn_sc[h, pl.ds(r0, SAMPLE_PAD), :] = u_sc[h, pl.ds(r0, SAMPLE_PAD), :] - r[:SAMPLE_PAD]
            qs_sc[h, pl.ds(r0, SAMPLE_PAD), :] = r[SAMPLE_PAD:]
        return carry

    lax.fori_loop(0, ns, state_read, 0)

    seq_of_row = lax.broadcasted_iota(jnp.int32, (rows, LANES), 0) >> int(math.log2(SAMPLE_PAD))

    def state_write(s, carry):
        r0 = pl.multiple_of(s * SAMPLE_PAD, SAMPLE_PAD)
        for h in range(HEADS):
            mine = jnp.where(seq_of_row == s, vn_sc[h], 0.0)
            ndn_ref[s, h] = sdn_ref[s, h] * eg_sc[h, pl.ds(r0, 1), :] + _bdot(kt_sc[h], mine)
        return carry

    lax.fori_loop(0, ns, state_write, 0)

    os_ = []
    for h in range(HEADS):
        o = qs_sc[h] + _bdot(attns[h], vn_sc[h])
        os_.append(_head_norm_gate(o, ng_ref))
    z = jnp.dot(u, win_ref[:, OFF_Z:OFF_Z + MIX], preferred_element_type=F32)
    o_dn = jnp.concatenate(os_, axis=1) * _silu(z)

    glu = jnp.dot(u, win_ref[:, OFF_GLU:OFF_GLU + 2 * MIX], preferred_element_type=F32)
    hh3 = (glu[:, :MIX] * jax.nn.sigmoid(glu[:, MIX:])).reshape(ns, SAMPLE_PAD, MIX)
    y = _conv_state(scm_ref[...], hh3, cws_ref, cwx_ref).reshape(rows, MIX) + cmb_ref[...]
    o_cm = _layernorm_swish(y, lng_ref, lnb_ref)
    ncm_ref[:, 0:CM_TAPS - 5, :] = scm_ref[:, 4:CM_TAPS - 1, :]
    ncm_ref[:, CM_TAPS - 5:CM_TAPS - 1, :] = hh3[:, 0:4, :]

    xo_ref[...] = _merge_out(x, u, (o_pool, o_dn, o_cm), win_ref, wbr_ref, wout_ref)


def _shifted_taps(w, prefix):
    taps, ch = w.shape
    ws = jnp.stack([jnp.concatenate([jnp.zeros((t, ch), F32), w[:prefix - t]], axis=0) for t in range(4)])
    wx = jnp.stack([jnp.concatenate([w[prefix - t:prefix + 1], jnp.zeros((SAMPLE_PAD - 1 - t, ch), F32)], axis=0)
                    for t in range(4)])
    return ws, wx


def _pool_tables():
    wdw = np.repeat(np.array(POOL_WINDOWS), LANES)[None, None, :]
    t = np.arange(4)[:, None, None]
    i = np.arange(POOL_BUF)[None, :, None]
    tau = np.arange(SAMPLE_PAD)[None, :, None]
    ms = (i >= POOL_BUF + 1 + t - wdw).astype(np.float32)
    mx = ((tau <= t) & (tau >= t - wdw + 1)).astype(np.float32)
    cnt = np.minimum(PAST_LEN + t + 1, wdw).astype(np.float32)
    return jnp.asarray(ms), jnp.asarray(mx), jnp.asarray(np.broadcast_to(cnt, (4, 1, MIX)).copy())


def _mixer_sample(x, st_pool, st_dnc, st_dn, st_cm, lw):
    ns = SAMPLE_SEQS
    rows = ns * SAMPLE_PAD
    nseq = st_pool.shape[0]
    weights = [lw[k] for k in ("g_mix", "w_in", "pool_w", "pool_scale", "dn_A_log", "dn_dt_bias", "dn_norm_g",
                               "cm_dw_b", "cm_ln_g", "cm_ln_b", "w_branch", "w_out")]
    tables = list(_pool_tables()) + list(_shifted_taps(lw["dn_conv_w"], DN_TAPS - 1)) \
        + list(_shifted_taps(lw["cm_dw_w"], CM_TAPS - 1))
    xspec = pl.BlockSpec((rows, D_MODEL), lambda i: (i, 0))

    def sspec(shape):
        nd = len(shape)
        return pl.BlockSpec((ns,) + tuple(shape[1:]), lambda i: (i,) + (0,) * (nd - 1))

    states = (st_pool, st_dnc, st_dn, st_cm)
    out_shapes = (jax.ShapeDtypeStruct(x.shape, F32),) + tuple(jax.ShapeDtypeStruct(s.shape, F32) for s in states)
    out_specs = (xspec,) + tuple(sspec(s.shape) for s in states)
    scratch = [pltpu.VMEM((HEADS, rows, HEAD_DIM), F32) for _ in range(7)]
    return pl.pallas_call(
        _mixer_sample_kernel,
        grid=(nseq // ns,),
        in_specs=[xspec] + [sspec(s.shape) for s in states] + [_resident(w.shape) for w in weights + tables],
        out_specs=out_specs,
        out_shape=out_shapes,
        scratch_shapes=scratch,
        compiler_params=pltpu.CompilerParams(dimension_semantics=("arbitrary",), vmem_limit_bytes=VMEM_LIMIT),
        name="mixer_sample",
    )(x, *states, *weights, *tables)


def _row(v):
    return v.reshape(1, -1).astype(F32)


def _lane_pad(v):
    return jnp.pad(v.astype(F32), (0, LANES - v.shape[0])).reshape(1, LANES)


def _pack_w_in(w):
    cut = 5 * MIX + 2 * HEADS
    pad = jnp.zeros((w.shape[0], OFF_GLU - cut), w.dtype)
    return jnp.concatenate([w[:, :cut], pad, w[:, cut:]], axis=1).astype(BF16)


def kernel(x_prompt, x_sample, state_pool, state_dn_conv, state_dn, state_cm_conv, p_prompt, p_sample, g_ffn1, w_ffn1_up, w_ffn1_down, g_mix, w_in, pool_w, pool_scale, dn_conv_w, dn_A_log, dn_dt_bias, dn_norm_g, cm_dw_w, cm_dw_b, cm_ln_g, cm_ln_b, w_branch, w_out, g_ffn2, w_ffn2_up, w_ffn2_down, g_ple, w_ple_gate, w_ple_proj, g_final):
    depth = g_ffn1.shape[0]
    bp, lp, _ = x_prompt.shape
    bs, ls, _ = x_sample.shape
    assert ls == 4 and lp % PROMPT_TILE == 0 and bs % SAMPLE_SEQS == 0
    hp = x_prompt.reshape(bp * lp, D_MODEL)
    hs = x_sample.reshape(bs * ls, D_MODEL)
    gfin = _row(g_final)
    outs_p = [[], [], [], []]
    outs_s = [[], [], [], []]
    for i in range(depth):
        lw = {"g_mix": _row(g_mix[i]), "w_in": _pack_w_in(w_in[i]), "pool_w": pool_w[i].astype(BF16),
              "pool_scale": _row(pool_scale[i]), "dn_conv_w": dn_conv_w[i], "dn_A_log": _lane_pad(dn_A_log[i]),
              "dn_dt_bias": _lane_pad(dn_dt_bias[i]), "dn_norm_g": _row(dn_norm_g[i]), "cm_dw_w": cm_dw_w[i],
              "cm_dw_b": _row(cm_dw_b[i]), "cm_ln_g": _row(cm_ln_g[i]), "cm_ln_b": _row(cm_ln_b[i]),
              "w_branch": w_branch[i].astype(BF16), "w_out": w_out[i].astype(BF16)}
        up1, dn1 = w_ffn1_up[i].astype(BF16), w_ffn1_down[i].astype(BF16)
        up2, dn2 = w_ffn2_up[i].astype(BF16), w_ffn2_down[i].astype(BF16)
        wgate, wproj = w_ple_gate[i].astype(BF16), w_ple_proj[i].astype(BF16)
        final = i == depth - 1

        hp = _ffn(hp, _row(g_ffn1[i]), up1, dn1)
        hp, a1, a2, a3, a4 = _mixer_prompt(hp.reshape(bp, lp, D_MODEL), lw)
        hp = _ffn_ple(hp.reshape(bp * lp, D_MODEL), p_prompt[i].reshape(bp * lp, PLE_DIM), _row(g_ffn2[i]),
                      up2, dn2, _row(g_ple[i]), wgate, wproj, gfin, final)

        hs = _ffn(hs, _row(g_ffn1[i]), up1, dn1)
        hs_pad = jnp.pad(hs.reshape(bs, ls, D_MODEL), ((0, 0), (0, SAMPLE_PAD - ls), (0, 0)))
        hs_pad, b1, b2, b3, b4 = _mixer_sample(hs_pad.reshape(bs * SAMPLE_PAD, D_MODEL), state_pool[i],
                                               state_dn_conv[i], state_dn[i], state_cm_conv[i], lw)
        hs = hs_pad.reshape(bs, SAMPLE_PAD, D_MODEL)[:, :ls].reshape(bs * ls, D_MODEL)
        hs = _ffn_ple(hs, p_sample[i].reshape(bs * ls, PLE_DIM), _row(g_ffn2[i]), up2, dn2, _row(g_ple[i]),
                      wgate, wproj, gfin, final)

        for acc, val in zip(outs_p, (a1, a2, a3, a4)):
            acc.append(val)
        for acc, val in zip(outs_s, (b1, b2, b3, b4)):
            acc.append(val)
    y_prompt = hp.reshape(bp, lp, D_MODEL)
    y_sample = hs.reshape(bs, ls, D_MODEL)
    return (y_prompt, y_sample) + tuple(jnp.stack(o) for o in outs_p) + tuple(jnp.stack(o) for o in outs_s)
```
